```python
import math
import jax, jax.numpy as jnp
from jax import lax
import numpy as np

D_MODEL = 1024
BATCH = 8
SEQ = 8192
DEPTH = 1

HEAD_DIM = 64
N_ATTN_HEADS = 8
N_KV_HEADS = 2
N_GMLP_GROUPS = 8
GMLP_GROUP_DIM = 64
ATTN_WIDTH = N_ATTN_HEADS * HEAD_DIM
KV_WIDTH = N_KV_HEADS * HEAD_DIM
GMLP_WIDTH = N_GMLP_GROUPS * GMLP_GROUP_DIM
MIX_WIDTH = ATTN_WIDTH + GMLP_WIDTH
IN_WIDTH = ATTN_WIDTH + 2 * KV_WIDTH + 2 * GMLP_WIDTH
WINDOW = 128
BLOCK = 128
CHUNK = 128
N_BUCKETS = 32
MAX_DISTANCE = 128
D_FF = -(-8 * D_MODEL // (3 * 256)) * 256
ALPHA = (2 * DEPTH) ** 0.25
BETA = (8 * DEPTH) ** -0.25
LN_EPS = 1e-5
NEG_INF = -1e30

kernel_name = "hymba_gmlp_swa_sink_deepnorm_adaln"


def layer_norm(x, g, b):
    xf = x.astype(jnp.float32)
    mu = jnp.mean(xf, axis=-1, keepdims=True)
    var = jnp.mean(jnp.square(xf - mu), axis=-1, keepdims=True)
    return ((xf - mu) * lax.rsqrt(var + LN_EPS) * g.astype(jnp.float32) + b.astype(jnp.float32)).astype(x.dtype)


def rms_norm(x, g):
    xf = x.astype(jnp.float32)
    ms = jnp.mean(jnp.square(xf), axis=-1, keepdims=True)
    return (xf * lax.rsqrt(ms + LN_EPS) * g.astype(jnp.float32)).astype(x.dtype)


def t5_bucket(dist):
    max_exact = N_BUCKETS // 2
    n = jnp.maximum(dist, 0)
    nf = jnp.maximum(n, max_exact).astype(jnp.float32)
    large = max_exact + (jnp.log(nf / max_exact) / math.log(MAX_DISTANCE / max_exact)
                         * (N_BUCKETS - max_exact)).astype(jnp.int32)
    large = jnp.minimum(large, N_BUCKETS - 1)
    return jnp.where(n < max_exact, n, large)


def sliding_window_attention(q, k, v, sinks, rel_bias):
    B, S, H, Dh = q.shape
    nb = S // BLOCK
    G = H // N_KV_HEADS
    qb = q.reshape(B, nb, BLOCK, N_KV_HEADS, G, Dh)
    kb = k.reshape(B, nb, BLOCK, N_KV_HEADS, Dh)
    vb = v.reshape(B, nb, BLOCK, N_KV_HEADS, Dh)
    kpad = jnp.zeros_like(kb[:, :1])
    vpad = jnp.zeros_like(vb[:, :1])
    kk = jnp.concatenate([jnp.concatenate([kpad, kb[:, :-1]], axis=1), kb], axis=2)
    vv = jnp.concatenate([jnp.concatenate([vpad, vb[:, :-1]], axis=1), vb], axis=2)
    logits = jnp.einsum('bnqkgd,bnskd->bnkgqs', qb, kk,
                        preferred_element_type=jnp.float32) * (Dh ** -0.5)
    qi = jnp.arange(BLOCK)[:, None]
    si = jnp.arange(2 * BLOCK)[None, :]
    dist = qi + BLOCK - si
    in_window = (dist >= 0) & (dist < WINDOW)
    bias = rel_bias.astype(jnp.float32)[t5_bucket(dist)]
    bias = bias.transpose(2, 0, 1).reshape(N_KV_HEADS, G, BLOCK, 2 * BLOCK)
    valid = in_window[None] & ((jnp.arange(nb)[:, None, None] > 0) | (si[None] >= BLOCK))
    logits = jnp.where(valid[None, :, None, None], logits + bias[None, None], NEG_INF)
    sink = sinks.astype(jnp.float32).reshape(N_KV_HEADS, G)[None, None, :, :, None, None]
    m = jnp.maximum(jnp.max(logits, axis=-1, keepdims=True), sink)
    p = jnp.exp(logits - m)
    p = p / (jnp.sum(p, axis=-1, keepdims=True) + jnp.exp(sink - m))
    out = jnp.einsum('bnkgqs,bnskd->bnqkgd', p.astype(v.dtype), vv)
    return out.reshape(B, S, H * Dh)


def chunked_spatial_gating(u, v, ln_g, ln_b, w_s, b_s):
    B, S, _ = u.shape
    nc = S // CHUNK
    G, Dg = N_GMLP_GROUPS, GMLP_GROUP_DIM
    u = jax.nn.gelu(u).reshape(B, nc, CHUNK, G, Dg)
    v = layer_norm(jax.nn.gelu(v).reshape(B, S, G, Dg), ln_g.reshape(G, Dg), ln_b.reshape(G, Dg))
    v = v.reshape(B, nc, CHUNK, G, Dg)
    causal = jnp.tril(jnp.ones((CHUNK, CHUNK), dtype=bool))
    w = jnp.where(causal[None], w_s, jnp.zeros_like(w_s))
    mixed = jnp.einsum('gts,bnsgc->bntgc', w, v) + b_s.T[None, None, :, :, None]
    return (u * mixed).reshape(B, S, GMLP_WIDTH)


def _normal(key, shape, scale):
    return jax.random.normal(key, shape, dtype=jnp.float32) * scale


def setup_inputs(seed: int = 0) -> dict:
    key = jax.random.key(seed)
    ks = jax.random.split(key, 24)
    L = DEPTH
    w_in = _normal(ks[5], (L, D_MODEL, IN_WIDTH), D_MODEL ** -0.5)
    v_lo, v_hi = ATTN_WIDTH + KV_WIDTH, ATTN_WIDTH + 2 * KV_WIDTH
    w_in = w_in.at[:, :, v_lo:v_hi].multiply(BETA)
    return {
        "x": _normal(ks[0], (BATCH, SEQ, D_MODEL), 1.0),
        "c": _normal(ks[1], (BATCH, D_MODEL), 1.0),
        "rel_bias": _normal(ks[2], (N_BUCKETS, N_ATTN_HEADS), 0.5),
        "w_ada": _normal(ks[3], (L, D_MODEL, 6 * D_MODEL), 0.5 * D_MODEL ** -0.5),
        "b_ada": _normal(ks[4], (L, 6 * D_MODEL), 0.01),
        "w_in": w_in,
        "b_in": _normal(ks[6], (L, IN_WIDTH), 0.01),
        "attn_sinks": _normal(ks[7], (L, N_ATTN_HEADS), 0.5),
        "gmlp_ln_g": 1.0 + _normal(ks[8], (L, GMLP_WIDTH), 0.01),
        "gmlp_ln_b": _normal(ks[9], (L, GMLP_WIDTH), 0.01),
        "gmlp_w_s": _normal(ks[10], (L, N_GMLP_GROUPS, CHUNK, CHUNK), CHUNK ** -0.5),
        "gmlp_b_s": 1.0 + _normal(ks[11], (L, N_GMLP_GROUPS, CHUNK), 0.01),
        "attn_out_g": 1.0 + _normal(ks[12], (L, ATTN_WIDTH), 0.01),
        "gmlp_out_g": 1.0 + _normal(ks[13], (L, GMLP_WIDTH), 0.01),
        "w_out": _normal(ks[14], (L, MIX_WIDTH, D_MODEL), BETA * MIX_WIDTH ** -0.5),
        "ln1_g": 1.0 + _normal(ks[15], (L, D_MODEL), 0.01),
        "ln1_b": _normal(ks[16], (L, D_MODEL), 0.01),
        "w_gate_up": _normal(ks[17], (L, D_MODEL, 2 * D_FF), D_MODEL ** -0.5),
        "w_down": _normal(ks[18], (L, D_FF, D_MODEL), BETA * D_FF ** -0.5),
        "ln2_g": 1.0 + _normal(ks[19], (L, D_MODEL), 0.01),
        "ln2_b": _normal(ks[20], (L, D_MODEL), 0.01),
    }


def reference(x, c, rel_bias, w_ada, b_ada, w_in, b_in, attn_sinks, gmlp_ln_g, gmlp_ln_b,
              gmlp_w_s, gmlp_b_s, attn_out_g, gmlp_out_g, w_out, ln1_g, ln1_b,
              w_gate_up, w_down, ln2_g, ln2_b):
    B, S, _ = x.shape
    splits = [ATTN_WIDTH, ATTN_WIDTH + KV_WIDTH, ATTN_WIDTH + 2 * KV_WIDTH,
              ATTN_WIDTH + 2 * KV_WIDTH + GMLP_WIDTH]
    for layer in range(DEPTH):
        mod = jax.nn.silu(c) @ w_ada[layer] + b_ada[layer]
        sh1, sc1, g1, sh2, sc2, g2 = jnp.split(mod[:, None, :], 6, axis=-1)

        h = x * (1.0 + sc1) + sh1
        proj = h @ w_in[layer] + b_in[layer]
        q, k, v, gu, gv = jnp.split(proj, splits, axis=-1)
        attn = sliding_window_attention(
            q.reshape(B, S, N_ATTN_HEADS, HEAD_DIM),
            k.reshape(B, S, N_KV_HEADS, HEAD_DIM),
            v.reshape(B, S, N_KV_HEADS, HEAD_DIM),
            attn_sinks[layer], rel_bias)
        gm = chunked_spatial_gating(gu, gv, gmlp_ln_g[layer], gmlp_ln_b[layer],
                                    gmlp_w_s[layer], gmlp_b_s[layer])
        mixed = jnp.concatenate([rms_norm(attn, attn_out_g[layer]),
                                 rms_norm(gm, gmlp_out_g[layer])], axis=-1)
        y = mixed @ w_out[layer]
        x = layer_norm(ALPHA * x + g1 * y, ln1_g[layer], ln1_b[layer])

        h = x * (1.0 + sc2) + sh2
        gate, up = jnp.split(h @ w_gate_up[layer], 2, axis=-1)
        y = (jax.nn.silu(gate) * up) @ w_down[layer]
        x = layer_norm(ALPHA * x + g2 * y, ln2_g[layer], ln2_b[layer])
    return x
```

```python
import functools
import math

import numpy as np
import jax
import jax.numpy as jnp
from jax import lax
from jax.experimental import pallas as pl
from jax.experimental.pallas import tpu as pltpu

HEAD_DIM = 64
N_KV_HEADS = 2
GMLP_GROUP_DIM = 64
BLOCK = 128
N_BUCKETS = 32
MAX_DISTANCE = 128
LN_EPS = 1e-5
NEG_INF = -1e30

LANES = 128
V7X_VMEM_LIMIT_BYTES = 56 * 1024 * 1024

TOKEN_TILE = 256
FF_CHUNK = 256

F32 = jnp.float32
BF16 = jnp.bfloat16


def _const_spec(shape):
    zeros = (0,) * len(shape)
    return pl.BlockSpec(shape, lambda *_: zeros, pipeline_mode=pl.Buffered(1))


def _layer_norm(z, g, b):
    mu = jnp.mean(z, axis=-1, keepdims=True)
    d = z - mu
    var = jnp.mean(d * d, axis=-1, keepdims=True)
    return d * lax.rsqrt(var + LN_EPS) * g + b


def _rms_norm(z, g):
    ms = jnp.mean(z * z, axis=-1, keepdims=True)
    return z * lax.rsqrt(ms + LN_EPS) * g


def _gelu_tanh(x):
    inner = math.sqrt(2.0 / math.pi) * (x + 0.044715 * (x * x * x))
    return 0.5 * x * (1.0 + jnp.tanh(inner))


def _silu(x):
    return x / (1.0 + jnp.exp(-x))


def _adaln_kernel(c_ref, w_ref, b_ref, o_ref):
    o_ref[...] = jnp.dot(_silu(c_ref[...]), w_ref[...], preferred_element_type=F32,
                         precision=lax.Precision.HIGHEST) + b_ref[...]


def _adaln_mod(c, w_ada, b_ada):
    batch, d_model = c.shape
    n_out = w_ada.shape[1]
    n_blk = n_out // 6
    return pl.pallas_call(
        _adaln_kernel,
        out_shape=jax.ShapeDtypeStruct((batch, n_out), F32),
        grid=(n_out // n_blk,),
        in_specs=[pl.BlockSpec((batch, d_model), lambda j: (0, 0)),
                  pl.BlockSpec((d_model, n_blk), lambda j: (0, j)),
                  pl.BlockSpec((1, n_blk), lambda j: (0, j))],
        out_specs=pl.BlockSpec((batch, n_blk), lambda j: (0, j)),
        compiler_params=pltpu.CompilerParams(dimension_semantics=("arbitrary",),
                                             vmem_limit_bytes=V7X_VMEM_LIMIT_BYTES),
        name="adaln_mod",
    )(c, w_ada, b_ada.reshape(1, n_out))


def _t5_bucket_table():
    qi = np.arange(BLOCK)[:, None]
    si = np.arange(2 * BLOCK)[None, :]
    n = np.maximum(qi + BLOCK - si, 0)
    max_exact = N_BUCKETS // 2
    nf = np.maximum(n, max_exact).astype(np.float64)
    large = max_exact + (np.log(nf / max_exact) / math.log(MAX_DISTANCE / max_exact)
                         * (N_BUCKETS - max_exact)).astype(np.int32)
    large = np.minimum(large, N_BUCKETS - 1)
    return np.where(n < max_exact, n, large).astype(np.int32)


def _tables_kernel(relb_ref, bucket_ref, ws_ref, tb_ref, wcat_ref, *, n_heads):
    bucket = bucket_ref[...]
    qi = lax.broadcasted_iota(jnp.int32, bucket.shape, 0)
    si = lax.broadcasted_iota(jnp.int32, bucket.shape, 1)
    dist = qi + BLOCK - si
    group = n_heads // N_KV_HEADS
    for h in range(n_heads):
        bias = jnp.zeros(bucket.shape, F32)
        for b in range(N_BUCKETS):
            bias = jnp.where(bucket == b, relb_ref[b * n_heads + h], bias)
        kv, c, half = h // group, (h % group) // 2, h % 2
        rows = slice(BLOCK * c, BLOCK * (c + 1))
        in_window = jnp.where(dist >= 0, jnp.where(dist < BLOCK, bias, NEG_INF), NEG_INF)
        tb_ref[0, kv, half, rows, :] = jnp.where(si >= BLOCK, in_window, NEG_INF)
        tb_ref[1, kv, half, rows, :] = in_window
    ti = lax.broadcasted_iota(jnp.int32, (BLOCK, BLOCK), 0)
    sj = lax.broadcasted_iota(jnp.int32, (BLOCK, BLOCK), 1)
    causal = ti >= sj
    for j in range(ws_ref.shape[0] // 2):
        wcat_ref[j, :, 0:BLOCK] = jnp.where(causal, ws_ref[2 * j], 0.0).astype(BF16)
        wcat_ref[j, :, BLOCK:2 * BLOCK] = jnp.where(causal, ws_ref[2 * j + 1], 0.0).astype(BF16)


def _mixer_tables(rel_bias, w_s):
    n_heads = rel_bias.shape[1]
    n_groups = w_s.shape[0]
    bucket = jnp.asarray(_t5_bucket_table())
    vmem = pl.BlockSpec(memory_space=pltpu.VMEM)
    return pl.pallas_call(
        functools.partial(_tables_kernel, n_heads=n_heads),
        out_shape=(jax.ShapeDtypeStruct((2, N_KV_HEADS, 2, 2 * BLOCK, 2 * BLOCK), F32),
                   jax.ShapeDtypeStruct((n_groups // 2, BLOCK, 2 * BLOCK), BF16)),
        in_specs=[pl.BlockSpec(memory_space=pltpu.SMEM), vmem, vmem],
        out_specs=(vmem, vmem),
        compiler_params=pltpu.CompilerParams(vmem_limit_bytes=V7X_VMEM_LIMIT_BYTES),
        name="mixer_tables",
    )(rel_bias.reshape(-1), bucket, w_s)


def _mixer_kernel(x_ref, mod_ref, w_in_ref, b_in_ref, tb_ref, sinks_ref, glg_ref, glb_ref,
                  wcat_ref, bs_ref, aog_ref, gog_ref, w_out_ref, l1g_ref, l1b_ref,
                  o_ref, kv_ref, *, alpha):
    tm = x_ref.shape[0]
    n_blk = tm // BLOCK
    attn_w = aog_ref.shape[1]
    kv_w = N_KV_HEADS * HEAD_DIM
    gm_w = gog_ref.shape[1]
    i = pl.program_id(1)

    x = x_ref[...]
    sh1, sc1, g1 = mod_ref[0:1, :], mod_ref[1:2, :], mod_ref[2:3, :]
    h = (x * (1.0 + sc1) + sh1).astype(BF16)
    proj = jnp.dot(h, w_in_ref[...], preferred_element_type=F32) + b_in_ref[...]

    q = (proj[:, 0:attn_w] * (HEAD_DIM ** -0.5)).astype(BF16)
    k = proj[:, attn_w:attn_w + kv_w]
    v = proj[:, attn_w + kv_w:attn_w + 2 * kv_w]
    gu = proj[:, attn_w + 2 * kv_w:attn_w + 2 * kv_w + gm_w]
    gv = proj[:, attn_w + 2 * kv_w + gm_w:attn_w + 2 * kv_w + 2 * gm_w]

    @pl.when(i == 0)
    def _():
        kv_ref[:, :, :, 0:BLOCK, :] = jnp.zeros((2, N_KV_HEADS, 2, BLOCK, LANES), BF16)

    low = lax.broadcasted_iota(jnp.int32, (tm, LANES), 1) < HEAD_DIM
    cur = slice(BLOCK, BLOCK + tm)
    for which, val in ((0, k), (1, v)):
        rolled = pltpu.roll(val, HEAD_DIM, axis=1)
        kv_ref[which, 0, 0, cur, :] = jnp.where(low, val, 0.0).astype(BF16)
        kv_ref[which, 0, 1, cur, :] = jnp.where(low, 0.0, rolled).astype(BF16)
        kv_ref[which, 1, 0, cur, :] = jnp.where(low, rolled, 0.0).astype(BF16)
        kv_ref[which, 1, 1, cur, :] = jnp.where(low, 0.0, val).astype(BF16)

    top = lax.broadcasted_iota(jnp.int32, (2 * BLOCK, 1), 0) < BLOCK
    nt = (((1,), (1,)), ((), ()))
    attn_blocks = []
    for t in range(n_blk):
        r0 = t * BLOCK
        first = jnp.where(i == 0, 0, 1) if t == 0 else 1
        cols = []
        for kvh in range(N_KV_HEADS):
            c0 = 2 * LANES * kvh
            q2 = jnp.concatenate([q[r0:r0 + BLOCK, c0:c0 + LANES],
                                  q[r0:r0 + BLOCK, c0 + LANES:c0 + 2 * LANES]], axis=0)
            att = None
            for half in range(2):
                kwin = kv_ref[0, kvh, half, r0:r0 + 2 * BLOCK, :]
                vwin = kv_ref[1, kvh, half, r0:r0 + 2 * BLOCK, :]
                s = lax.dot_general(q2, kwin, nt, preferred_element_type=F32)
                s = s + tb_ref[first, kvh, half]
                h0 = 4 * kvh + half
                sink = jnp.where(top, sinks_ref[h0], sinks_ref[h0 + 2])
                m = jnp.maximum(jnp.max(s, axis=-1, keepdims=True), sink)
                p = jnp.exp(s - m)
                denom = jnp.sum(p, axis=-1, keepdims=True) + jnp.exp(sink - m)
                o = jnp.dot(p.astype(BF16), vwin, preferred_element_type=F32) / denom
                att = o if att is None else att + o
            cols += [att[0:BLOCK], att[BLOCK:2 * BLOCK]]
        attn_blocks.append(jnp.concatenate(cols, axis=1))
    attn = jnp.concatenate(attn_blocks, axis=0)

    u = _gelu_tanh(gu)
    gvv = _gelu_tanh(gv)
    inv_g = 1.0 / GMLP_GROUP_DIM
    gm_cols = []
    for j in range(gm_w // LANES):
        cj = slice(LANES * j, LANES * (j + 1))
        xj = gvv[:, cj]
        s_low = jnp.sum(jnp.where(low, xj, 0.0), axis=-1, keepdims=True)
        s_all = jnp.sum(xj, axis=-1, keepdims=True)
        dj = xj - jnp.where(low, s_low, s_all - s_low) * inv_g
        d2 = dj * dj
        v_low = jnp.sum(jnp.where(low, d2, 0.0), axis=-1, keepdims=True)
        v_all = jnp.sum(d2, axis=-1, keepdims=True)
        var = jnp.where(low, v_low, v_all - v_low) * inv_g
        vn = dj * lax.rsqrt(var + LN_EPS) * glg_ref[:, cj] + glb_ref[:, cj]
        vn_low = jnp.where(low, vn, 0.0).astype(BF16)
        vn_high = jnp.where(low, 0.0, vn).astype(BF16)
        rhs = jnp.concatenate(
            [jnp.concatenate([vn_low[c * BLOCK:(c + 1) * BLOCK], vn_high[c * BLOCK:(c + 1) * BLOCK]],
                             axis=0) for c in range(n_blk)], axis=1)
        mixed = jnp.dot(wcat_ref[j], rhs, preferred_element_type=F32)
        mixed = jnp.concatenate([mixed[:, c * LANES:(c + 1) * LANES] for c in range(n_blk)], axis=0)
        bias = jnp.concatenate([bs_ref[:, cj]] * n_blk, axis=0)
        gm_cols.append(u[:, cj] * (mixed + bias))
    gm = jnp.concatenate(gm_cols, axis=1)

    mixed_all = jnp.concatenate([_rms_norm(attn, aog_ref[...]), _rms_norm(gm, gog_ref[...])],
                                axis=1).astype(BF16)
    y = jnp.dot(mixed_all, w_out_ref[...], preferred_element_type=F32)
    o_ref[...] = _layer_norm(alpha * x + g1 * y, l1g_ref[...], l1b_ref[...])

    kv_ref[:, :, :, 0:BLOCK, :] = kv_ref[:, :, :, tm:tm + BLOCK, :]


def _mixer(x, mod, w_in, b_in, tb, sinks, gln_g, gln_b, wcat, bs, aog, gog, w_out, l1g, l1b, *, alpha):
    batch, seq, d_model = x.shape
    tm = TOKEN_TILE
    row = lambda a: a.reshape(1, -1)
    tile = pl.BlockSpec((None, tm, d_model), lambda b, i: (b, i, 0))
    consts = [w_in, row(b_in), tb, None, row(gln_g), row(gln_b), wcat, bs, row(aog), row(gog),
              w_out, row(l1g), row(l1b)]
    const_specs = [pl.BlockSpec(memory_space=pltpu.SMEM) if a is None else _const_spec(a.shape)
                   for a in consts]
    consts[3] = sinks
    return pl.pallas_call(
        functools.partial(_mixer_kernel, alpha=alpha),
        out_shape=jax.ShapeDtypeStruct(x.shape, F32),
        grid=(batch, seq // tm),
        in_specs=[tile, pl.BlockSpec((None, 6, d_model), lambda b, i: (b, 0, 0))] + const_specs,
        out_specs=tile,
        scratch_shapes=[pltpu.VMEM((2, N_KV_HEADS, 2, BLOCK + tm, LANES), BF16)],
        compiler_params=pltpu.CompilerParams(dimension_semantics=("arbitrary", "arbitrary"),
                                             vmem_limit_bytes=V7X_VMEM_LIMIT_BYTES),
        name="mixer",
    )(x, mod, *consts)


def _ffn_kernel(x_ref, mod_ref, wgu_ref, wd_ref, l2g_ref, l2b_ref, o_ref, act_ref, *, alpha):
    d_ff = wd_ref.shape[0]
    x = x_ref[...]
    sh2, sc2, g2 = mod_ref[3:4, :], mod_ref[4:5, :], mod_ref[5:6, :]
    h = (x * (1.0 + sc2) + sh2).astype(BF16)
    for c0 in range(0, d_ff, FF_CHUNK):
        gate = jnp.dot(h, wgu_ref[:, c0:c0 + FF_CHUNK], preferred_element_type=F32)
        up = jnp.dot(h, wgu_ref[:, d_ff + c0:d_ff + c0 + FF_CHUNK], preferred_element_type=F32)
        act_ref[:, c0:c0 + FF_CHUNK] = (_silu(gate) * up).astype(BF16)
    y = jnp.dot(act_ref[...], wd_ref[...], preferred_element_type=F32)
    o_ref[...] = _layer_norm(alpha * x + g2 * y, l2g_ref[...], l2b_ref[...])


def _ffn(x, mod, w_gate_up, w_down, l2g, l2b, *, alpha):
    batch, seq, d_model = x.shape
    d_ff = w_down.shape[0]
    tm = TOKEN_TILE
    row = lambda a: a.reshape(1, -1)
    tile = pl.BlockSpec((None, tm, d_model), lambda b, i: (b, i, 0))
    consts = [w_gate_up, w_down, row(l2g), row(l2b)]
    return pl.pallas_call(
        functools.partial(_ffn_kernel, alpha=alpha),
        out_shape=jax.ShapeDtypeStruct(x.shape, F32),
        grid=(batch, seq // tm),
        in_specs=[tile, pl.BlockSpec((None, 6, d_model), lambda b, i: (b, 0, 0))]
                 + [_const_spec(a.shape) for a in consts],
        out_specs=tile,
        scratch_shapes=[pltpu.VMEM((tm, d_ff), BF16)],
        compiler_params=pltpu.CompilerParams(dimension_semantics=("arbitrary", "arbitrary"),
                                             vmem_limit_bytes=V7X_VMEM_LIMIT_BYTES),
        name="ffn",
    )(x, mod, *consts)


def kernel(x, c, rel_bias, w_ada, b_ada, w_in, b_in, attn_sinks, gmlp_ln_g, gmlp_ln_b,
           gmlp_w_s, gmlp_b_s, attn_out_g, gmlp_out_g, w_out, ln1_g, ln1_b,
           w_gate_up, w_down, ln2_g, ln2_b):
    batch, seq, d_model = x.shape
    depth = w_in.shape[0]
    assert seq % TOKEN_TILE == 0 and TOKEN_TILE % BLOCK == 0
    assert w_down.shape[1] % FF_CHUNK == 0
    alpha = (2 * depth) ** 0.25
    for layer in range(depth):
        mod = _adaln_mod(c, w_ada[layer], b_ada[layer]).reshape(batch, 6, d_model)
        tb, wcat = _mixer_tables(rel_bias, gmlp_w_s[layer])
        bs = jnp.repeat(gmlp_b_s[layer].T, GMLP_GROUP_DIM, axis=1)
        x = _mixer(x, mod, w_in[layer].astype(BF16), b_in[layer], tb, attn_sinks[layer],
                   gmlp_ln_g[layer], gmlp_ln_b[layer], wcat, bs, attn_out_g[layer],
                   gmlp_out_g[layer], w_out[layer].astype(BF16), ln1_g[layer], ln1_b[layer],
                   alpha=alpha)
        x = _ffn(x, mod, w_gate_up[layer].astype(BF16), w_down[layer].astype(BF16),
                 ln2_g[layer], ln2_b[layer], alpha=alpha)
    return x
```

```python
import functools
import math

import numpy as np
import jax
import jax.numpy as jnp
from jax import lax
from jax.experimental import pallas as pl
from jax.experimental.pallas import tpu as pltpu

HEAD_DIM = 64
N_KV_HEADS = 2
GMLP_GROUP_DIM = 64
BLOCK = 128
N_BUCKETS = 32
MAX_DISTANCE = 128
LN_EPS = 1e-5
NEG_INF = -1e30
LOG2E = math.log2(math.e)

LANES = 128
MXU_COLS = 256
V7X_VMEM_LIMIT_BYTES = 56 * 1024 * 1024

QK_PER_UNIT = 2
PV_LAG = 2
G_PER_UNIT = 2
NORM_LAG = 2
MIXER_TILE = 512
FFN_TILE = 512
SUB_TILE = 256
FF_CHUNK = 256
MIXER_SKEW = 6
FFN_SKEW = 8

F32 = jnp.float32
BF16 = jnp.bfloat16


def _const_spec(shape):
    zeros = (0,) * len(shape)
    return pl.BlockSpec(shape, lambda *_: zeros, pipeline_mode=pl.Buffered(1))


def _layer_norm(z, g, b):
    mu = jnp.mean(z, axis=-1, keepdims=True)
    d = z - mu
    var = jnp.mean(d * d, axis=-1, keepdims=True)
    return d * lax.rsqrt(var + LN_EPS) * g + b


def _rms_norm(z, g):
    ms = jnp.mean(z * z, axis=-1, keepdims=True)
    return z * lax.rsqrt(ms + LN_EPS) * g


def _gelu_tanh(x):
    k0 = -2.0 * math.sqrt(2.0 / math.pi) * LOG2E
    return x / (1.0 + jnp.exp2(x * (k0 + (0.044715 * k0) * (x * x))))


def _silu(x):
    return x / (1.0 + jnp.exp2(x * -LOG2E))


def _lane_tiles(val):
    return [val[:, c:c + LANES] for c in range(0, val.shape[1], LANES)]


def _interleave(chains, skew):
    chains = list(chains)
    live = [True] * len(chains)
    rnd = 0
    while any(live):
        for k, chain in enumerate(chains):
            if live[k] and rnd >= k * skew:
                live[k] = next(chain, "done") != "done"
        rnd += 1


def _adaln_kernel(c_ref, w_ref, b_ref, o_ref):
    o_ref[...] = jnp.dot(_silu(c_ref[...]), w_ref[...], preferred_element_type=F32,
                         precision=lax.Precision.HIGHEST) + b_ref[...]


def _adaln_mod(c, w_ada, b_ada):
    batch, d_model = c.shape
    n_out = w_ada.shape[1]
    n_blk = n_out // 6
    return pl.pallas_call(
        _adaln_kernel,
        out_shape=jax.ShapeDtypeStruct((batch, n_out), F32),
        grid=(n_out // n_blk,),
        in_specs=[pl.BlockSpec((batch, d_model), lambda j: (0, 0)),
                  pl.BlockSpec((d_model, n_blk), lambda j: (0, j)),
                  pl.BlockSpec((1, n_blk), lambda j: (0, j))],
        out_specs=pl.BlockSpec((batch, n_blk), lambda j: (0, j)),
        compiler_params=pltpu.CompilerParams(dimension_semantics=("arbitrary",),
                                             vmem_limit_bytes=V7X_VMEM_LIMIT_BYTES),
        name="adaln_mod",
    )(c, w_ada, b_ada.reshape(1, n_out))


def _t5_bucket_table():
    qi = np.arange(BLOCK)[:, None]
    si = np.arange(2 * BLOCK)[None, :]
    n = np.maximum(qi + BLOCK - si, 0)
    max_exact = N_BUCKETS // 2
    nf = np.maximum(n, max_exact).astype(np.float64)
    large = max_exact + (np.log(nf / max_exact) / math.log(MAX_DISTANCE / max_exact)
                         * (N_BUCKETS - max_exact)).astype(np.int32)
    large = np.minimum(large, N_BUCKETS - 1)
    return np.where(n < max_exact, n, large).astype(np.int32)


def _tables_kernel(relb_ref, bucket_ref, ws_ref, tb_ref, wcat_ref, *, n_heads):
    bucket = bucket_ref[...]
    qi = lax.broadcasted_iota(jnp.int32, bucket.shape, 0)
    si = lax.broadcasted_iota(jnp.int32, bucket.shape, 1)
    dist = qi + BLOCK - si
    group = n_heads // N_KV_HEADS
    for h in range(n_heads):
        bias = jnp.zeros(bucket.shape, F32)
        for b in range(N_BUCKETS):
            bias = jnp.where(bucket == b, relb_ref[b * n_heads + h] * LOG2E, bias)
        kv, c, half = h // group, (h % group) // 2, h % 2
        rows = slice(BLOCK * c, BLOCK * (c + 1))
        in_window = jnp.where(dist >= 0, jnp.where(dist < BLOCK, bias, NEG_INF), NEG_INF)
        tb_ref[0, kv, half, rows, :] = jnp.where(si >= BLOCK, in_window, NEG_INF)
        tb_ref[1, kv, half, rows, :] = in_window
    ti = lax.broadcasted_iota(jnp.int32, (BLOCK, BLOCK), 0)
    sj = lax.broadcasted_iota(jnp.int32, (BLOCK, BLOCK), 1)
    causal = ti >= sj
    for j in range(ws_ref.shape[0] // 2):
        wcat_ref[j, :, 0:BLOCK] = jnp.where(causal, ws_ref[2 * j], 0.0).astype(BF16)
        wcat_ref[j, :, BLOCK:2 * BLOCK] = jnp.where(causal, ws_ref[2 * j + 1], 0.0).astype(BF16)


def _mixer_tables(rel_bias, w_s):
    n_heads = rel_bias.shape[1]
    n_groups = w_s.shape[0]
    bucket = jnp.asarray(_t5_bucket_table())
    vmem = pl.BlockSpec(memory_space=pltpu.VMEM)
    return pl.pallas_call(
        functools.partial(_tables_kernel, n_heads=n_heads),
        out_shape=(jax.ShapeDtypeStruct((2, N_KV_HEADS, 2, 2 * BLOCK, 2 * BLOCK), F32),
                   jax.ShapeDtypeStruct((n_groups // 2, BLOCK, 2 * BLOCK), BF16)),
        in_specs=[pl.BlockSpec(memory_space=pltpu.SMEM), vmem, vmem],
        out_specs=(vmem, vmem),
        compiler_params=pltpu.CompilerParams(vmem_limit_bytes=V7X_VMEM_LIMIT_BYTES),
        name="mixer_tables",
    )(rel_bias.reshape(-1), bucket, w_s)


def _mixer_kernel(x_ref, mod_ref, w_in_ref, b_in_ref, tb_ref, sinks_ref, glg_ref, glb_ref,
                  wcat_ref, bs_ref, aog_ref, gog_ref, w_out_ref, l1g_ref, l1b_ref,
                  o_ref, kv_ref, *, alpha):
    tm = x_ref.shape[0]
    i = pl.program_id(1)
    refs = (x_ref, mod_ref, w_in_ref, b_in_ref, tb_ref, sinks_ref, glg_ref, glb_ref, wcat_ref,
            bs_ref, aog_ref, gog_ref, w_out_ref, l1g_ref, l1b_ref, o_ref, kv_ref)

    @pl.when(i == 0)
    def _():
        kv_ref[:, :, :, 0:BLOCK, :] = jnp.zeros((2, N_KV_HEADS, 2, BLOCK, LANES), BF16)

    _interleave([_mixer_chain(sub * SUB_TILE, i == 0, refs, alpha)
                 for sub in range(tm // SUB_TILE)], MIXER_SKEW)

    kv_ref[:, :, :, 0:BLOCK, :] = kv_ref[:, :, :, tm:tm + BLOCK, :]


def _mixer_chain(row0, seq_start, refs, alpha):
    (x_ref, mod_ref, w_in_ref, b_in_ref, tb_ref, sinks_ref, glg_ref, glb_ref, wcat_ref,
     bs_ref, aog_ref, gog_ref, w_out_ref, l1g_ref, l1b_ref, o_ref, kv_ref) = refs
    tm = SUB_TILE
    n_blk = tm // BLOCK
    d_model = x_ref.shape[1]
    attn_w = aog_ref.shape[1]
    kv_w = N_KV_HEADS * HEAD_DIM
    gm_w = gog_ref.shape[1]
    rows = slice(row0, row0 + tm)
    low = lax.broadcasted_iota(jnp.int32, (tm, LANES), 1) < HEAD_DIM
    inv_g = 1.0 / GMLP_GROUP_DIM

    x = x_ref[rows, :]
    sh1, sc1, g1 = mod_ref[0:1, :], mod_ref[1:2, :], mod_ref[2:3, :]
    h = (x * (1.0 + sc1) + sh1).astype(BF16)
    yield

    def proj(c0, width):
        return (jnp.dot(h, w_in_ref[:, c0:c0 + width], preferred_element_type=F32)
                + b_in_ref[:, c0:c0 + width])

    q_tiles = []
    for c0 in range(0, attn_w, MXU_COLS):
        q_tiles += _lane_tiles((proj(c0, MXU_COLS) * (LOG2E * HEAD_DIM ** -0.5)).astype(BF16))
        yield

    kv_new = proj(attn_w, 2 * kv_w)
    cur = slice(BLOCK + row0, BLOCK + row0 + tm)
    for which in range(2):
        val = kv_new[:, which * kv_w:(which + 1) * kv_w]
        rolled = pltpu.roll(val, HEAD_DIM, axis=1)
        kv_ref[which, 0, 0, cur, :] = jnp.where(low, val, 0.0).astype(BF16)
        kv_ref[which, 0, 1, cur, :] = jnp.where(low, 0.0, rolled).astype(BF16)
        kv_ref[which, 1, 0, cur, :] = jnp.where(low, rolled, 0.0).astype(BF16)
        kv_ref[which, 1, 1, cur, :] = jnp.where(low, 0.0, val).astype(BF16)
    yield

    gu0 = attn_w + 2 * kv_w
    gv0 = gu0 + gm_w
    rhs_tiles = []
    for c0 in range(gv0, gv0 + gm_w, MXU_COLS):
        for jj, xj in enumerate(_lane_tiles(_gelu_tanh(proj(c0, MXU_COLS)))):
            cj = slice(c0 - gv0 + jj * LANES, c0 - gv0 + (jj + 1) * LANES)
            s_low = jnp.sum(jnp.where(low, xj, 0.0), axis=-1, keepdims=True)
            s_all = jnp.sum(xj, axis=-1, keepdims=True)
            dj = xj - jnp.where(low, s_low, s_all - s_low) * inv_g
            d2 = dj * dj
            v_low = jnp.sum(jnp.where(low, d2, 0.0), axis=-1, keepdims=True)
            v_all = jnp.sum(d2, axis=-1, keepdims=True)
            var = jnp.where(low, v_low, v_all - v_low) * inv_g
            vn = dj * lax.rsqrt(var + LN_EPS) * glg_ref[:, cj] + glb_ref[:, cj]
            vn_low = jnp.where(low, vn, 0.0).astype(BF16)
            vn_high = jnp.where(low, 0.0, vn).astype(BF16)
            rhs_tiles.append(jnp.concatenate(
                [jnp.concatenate([vn_low[c * BLOCK:(c + 1) * BLOCK],
                                  vn_high[c * BLOCK:(c + 1) * BLOCK]], axis=0)
                 for c in range(n_blk)], axis=1))
        yield

    u_tiles = []
    for c0 in range(gu0, gu0 + gm_w, MXU_COLS):
        u_tiles += _lane_tiles(_gelu_tanh(proj(c0, MXU_COLS)))
        yield

    top = lax.broadcasted_iota(jnp.int32, (2 * BLOCK, 1), 0) < BLOCK
    nt = (((1,), (1,)), ((), ()))
    heads = [(t, kvh, half) for t in range(n_blk) for kvh in range(N_KV_HEADS) for half in range(2)]
    n_lane = gm_w // LANES
    scores, probs, att_parts, gm_tiles = {}, {}, {}, []
    for unit in range(-(-len(heads) // QK_PER_UNIT) + PV_LAG):
        qk_now = heads[unit * QK_PER_UNIT:(unit + 1) * QK_PER_UNIT]
        pv_now = heads[(unit - PV_LAG) * QK_PER_UNIT:(unit - PV_LAG + 1) * QK_PER_UNIT] \
            if unit >= PV_LAG else []
        g_now = list(range(unit * G_PER_UNIT, min((unit + 1) * G_PER_UNIT, n_lane)))
        for t, kvh, half in qk_now:
            r0 = t * BLOCK
            q2 = jnp.concatenate([q_tiles[2 * kvh][r0:r0 + BLOCK],
                                  q_tiles[2 * kvh + 1][r0:r0 + BLOCK]], axis=0)
            kwin = kv_ref[0, kvh, half, row0 + r0:row0 + r0 + 2 * BLOCK, :]
            scores[t, kvh, half] = lax.dot_general(q2, kwin, nt, preferred_element_type=F32)
        pv_out = {}
        for t, kvh, half in pv_now:
            p, denom = probs.pop((t, kvh, half))
            vwin = kv_ref[1, kvh, half, row0 + t * BLOCK:row0 + (t + 2) * BLOCK, :]
            pv_out[t, kvh, half] = (jnp.dot(p, vwin, preferred_element_type=F32), denom)
        mixed_now = {j: jnp.dot(wcat_ref[j], rhs_tiles[j], preferred_element_type=F32)
                     for j in g_now}
        for t, kvh, half in qk_now:
            first = jnp.where(seq_start, 0, 1) if row0 + t * BLOCK == 0 else 1
            s = scores.pop((t, kvh, half)) + tb_ref[first, kvh, half]
            h0 = 4 * kvh + half
            sink = jnp.where(top, sinks_ref[h0] * LOG2E, sinks_ref[h0 + 2] * LOG2E)
            m = jnp.maximum(jnp.max(s, axis=-1, keepdims=True), sink)
            p = jnp.exp2(s - m)
            denom = jnp.sum(p, axis=-1, keepdims=True) + jnp.exp2(sink - m)
            probs[t, kvh, half] = (p.astype(BF16), denom)
        for key, (o, denom) in pv_out.items():
            att_parts[key] = o / denom
        for j in g_now:
            cj = slice(LANES * j, LANES * (j + 1))
            mixed = jnp.concatenate(_lane_tiles(mixed_now[j]), axis=0)
            bias = jnp.concatenate([bs_ref[:, cj]] * n_blk, axis=0)
            gm_tiles.append(u_tiles[j] * (mixed + bias))
        yield
    assert len(gm_tiles) == n_lane and not probs
    attn_blocks = []
    for t in range(n_blk):
        cols = []
        for kvh in range(N_KV_HEADS):
            att = att_parts[t, kvh, 0] + att_parts[t, kvh, 1]
            cols += [att[0:BLOCK], att[BLOCK:2 * BLOCK]]
        attn_blocks.append(jnp.concatenate(cols, axis=1))
    attn = jnp.concatenate(attn_blocks, axis=0)
    gm = jnp.concatenate(gm_tiles, axis=1)
    mixed_all = jnp.concatenate([_rms_norm(attn, aog_ref[...]), _rms_norm(gm, gog_ref[...])],
                                axis=1).astype(BF16)
    for _ in range(NORM_LAG):
        yield

    z_parts = []
    for c0 in range(0, d_model, MXU_COLS):
        cs = slice(c0, c0 + MXU_COLS)
        y = jnp.dot(mixed_all, w_out_ref[:, cs], preferred_element_type=F32)
        z_parts.append(alpha * x[:, cs] + g1[:, cs] * y)
        yield
    o_ref[rows, :] = _layer_norm(jnp.concatenate(z_parts, axis=1), l1g_ref[...], l1b_ref[...])
    yield


def _mixer(x, mod, w_in, b_in, tb, sinks, gln_g, gln_b, wcat, bs, aog, gog, w_out, l1g, l1b, *, alpha):
    batch, seq, d_model = x.shape
    tm = MIXER_TILE
    row = lambda a: a.reshape(1, -1)
    tile = pl.BlockSpec((None, tm, d_model), lambda b, i: (b, i, 0))
    consts = [w_in, row(b_in), tb, None, row(gln_g), row(gln_b), wcat, bs, row(aog), row(gog),
              w_out, row(l1g), row(l1b)]
    const_specs = [pl.BlockSpec(memory_space=pltpu.SMEM) if a is None else _const_spec(a.shape)
                   for a in consts]
    consts[3] = sinks
    return pl.pallas_call(
        functools.partial(_mixer_kernel, alpha=alpha),
        out_shape=jax.ShapeDtypeStruct(x.shape, F32),
        grid=(batch, seq // tm),
        in_specs=[tile, pl.BlockSpec((None, 6, d_model), lambda b, i: (b, 0, 0))] + const_specs,
        out_specs=tile,
        scratch_shapes=[pltpu.VMEM((2, N_KV_HEADS, 2, BLOCK + tm, LANES), BF16)],
        compiler_params=pltpu.CompilerParams(dimension_semantics=("arbitrary", "arbitrary"),
                                             vmem_limit_bytes=V7X_VMEM_LIMIT_BYTES),
        name="mixer",
    )(x, mod, *consts)


def _ffn_kernel(x_ref, mod_ref, wgu_ref, wd_ref, l2g_ref, l2b_ref, o_ref, act_ref, *, alpha):
    refs = (x_ref, mod_ref, wgu_ref, wd_ref, l2g_ref, l2b_ref, o_ref, act_ref)
    _interleave([_ffn_chain(sub * SUB_TILE, refs, alpha)
                 for sub in range(x_ref.shape[0] // SUB_TILE)], FFN_SKEW)


def _ffn_chain(row0, refs, alpha):
    x_ref, mod_ref, wgu_ref, wd_ref, l2g_ref, l2b_ref, o_ref, act_ref = refs
    d_ff, d_model = wd_ref.shape
    rows = slice(row0, row0 + SUB_TILE)
    x = x_ref[rows, :]
    sh2, sc2, g2 = mod_ref[3:4, :], mod_ref[4:5, :], mod_ref[5:6, :]
    h = (x * (1.0 + sc2) + sh2).astype(BF16)
    yield
    for c0 in range(0, d_ff, FF_CHUNK):
        gate = jnp.dot(h, wgu_ref[:, c0:c0 + FF_CHUNK], preferred_element_type=F32)
        up = jnp.dot(h, wgu_ref[:, d_ff + c0:d_ff + c0 + FF_CHUNK], preferred_element_type=F32)
        act_ref[rows, c0:c0 + FF_CHUNK] = (_silu(gate) * up).astype(BF16)
        yield
    z_parts = []
    for c0 in range(0, d_model, MXU_COLS):
        cs = slice(c0, c0 + MXU_COLS)
        y = jnp.dot(act_ref[rows, :], wd_ref[:, cs], preferred_element_type=F32)
        z_parts.append(alpha * x[:, cs] + g2[:, cs] * y)
        yield
    o_ref[rows, :] = _layer_norm(jnp.concatenate(z_parts, axis=1), l2g_ref[...], l2b_ref[...])
    yield


def _ffn(x, mod, w_gate_up, w_down, l2g, l2b, *, alpha):
    batch, seq, d_model = x.shape
    d_ff = w_down.shape[0]
    tm = FFN_TILE
    row = lambda a: a.reshape(1, -1)
    tile = pl.BlockSpec((None, tm, d_model), lambda b, i: (b, i, 0))
    consts = [w_gate_up, w_down, row(l2g), row(l2b)]
    return pl.pallas_call(
        functools.partial(_ffn_kernel, alpha=alpha),
        out_shape=jax.ShapeDtypeStruct(x.shape, F32),
        grid=(batch, seq // tm),
        in_specs=[tile, pl.BlockSpec((None, 6, d_model), lambda b, i: (b, 0, 0))]
                 + [_const_spec(a.shape) for a in consts],
        out_specs=tile,
        scratch_shapes=[pltpu.VMEM((tm, d_ff), BF16)],
        compiler_params=pltpu.CompilerParams(dimension_semantics=("arbitrary", "arbitrary"),
                                             vmem_limit_bytes=V7X_VMEM_LIMIT_BYTES),
        name="ffn",
    )(x, mod, *consts)


def kernel(x, c, rel_bias, w_ada, b_ada, w_in, b_in, attn_sinks, gmlp_ln_g, gmlp_ln_b,
           gmlp_w_s, gmlp_b_s, attn_out_g, gmlp_out_g, w_out, ln1_g, ln1_b,
           w_gate_up, w_down, ln2_g, ln2_b):
    batch, seq, d_model = x.shape
    depth = w_in.shape[0]
    for tile in (MIXER_TILE, FFN_TILE):
        assert seq % tile == 0 and tile % SUB_TILE == 0 and SUB_TILE % BLOCK == 0
    assert w_down.shape[1] % FF_CHUNK == 0
    alpha = (2 * depth) ** 0.25
    for layer in range(depth):
        mod = _adaln_mod(c, w_ada[layer], b_ada[layer]).reshape(batch, 6, d_model)
        tb, wcat = _mixer_tables(rel_bias, gmlp_w_s[layer])
        bs = jnp.repeat(gmlp_b_s[layer].T, GMLP_GROUP_DIM, axis=1)
        x = _mixer(x, mod, w_in[layer].astype(BF16), b_in[layer], tb, attn_sinks[layer],
                   gmlp_ln_g[layer], gmlp_ln_b[layer], wcat, bs, attn_out_g[layer],
                   gmlp_out_g[layer], w_out[layer].astype(BF16), ln1_g[layer], ln1_b[layer],
                   alpha=alpha)
        x = _ffn(x, mod, w_gate_up[layer].astype(BF16), w_down[layer].astype(BF16),
                 ln2_g[layer], ln2_b[layer], alpha=alpha)
    return x
```

```python
import functools
import math

import numpy as np
import jax
import jax.numpy as jnp
from jax import lax
from jax.experimental import pallas as pl
from jax.experimental.pallas import tpu as pltpu

HEAD_DIM = 64
N_KV_HEADS = 2
GMLP_GROUP_DIM = 64
BLOCK = 128
N_BUCKETS = 32
MAX_DISTANCE = 128
LN_EPS = 1e-5
NEG_INF = -1e30
LOG2E = math.log2(math.e)

LANES = 128
MXU_COLS = 256
V7X_VMEM_LIMIT_BYTES = 56 * 1024 * 1024

QK_PER_UNIT = 2
PV_LAG = 2
G_PER_UNIT = 2
NORM_LAG = 2
LAYER_TILE = 512
SUB_TILE = 256
FF_CHUNK = 256
MIXER_STAGGER = 4
FFN_STAGGER = 0

F32 = jnp.float32
BF16 = jnp.bfloat16


def _const_spec(shape):
    zeros = (0,) * len(shape)
    return pl.BlockSpec(shape, lambda *_: zeros, pipeline_mode=pl.Buffered(1))


def _layer_norm(z, g, b):
    mu = jnp.mean(z, axis=-1, keepdims=True)
    d = z - mu
    var = jnp.mean(d * d, axis=-1, keepdims=True)
    return d * lax.rsqrt(var + LN_EPS) * g + b


def _rms_norm(z, g):
    ms = jnp.mean(z * z, axis=-1, keepdims=True)
    return z * lax.rsqrt(ms + LN_EPS) * g


def _gelu_tanh(x):
    k0 = -2.0 * math.sqrt(2.0 / math.pi) * LOG2E
    return x / (1.0 + jnp.exp2(x * (k0 + (0.044715 * k0) * (x * x))))


def _silu(x):
    return x / (1.0 + jnp.exp2(x * -LOG2E))


def _lane_tiles(val):
    return [val[:, c:c + LANES] for c in range(0, val.shape[1], LANES)]


def _interleave(chains, starts):
    chains = list(chains)
    live = [True] * len(chains)
    rnd = 0
    while any(live):
        for k, chain in enumerate(chains):
            if live[k] and rnd >= starts[k]:
                live[k] = next(chain, "done") != "done"
        rnd += 1


def _adaln_kernel(c_ref, w_ref, b_ref, o_ref):
    o_ref[...] = jnp.dot(_silu(c_ref[...]), w_ref[...], preferred_element_type=F32,
                         precision=lax.Precision.HIGHEST) + b_ref[...]


def _adaln_mod(c, w_ada, b_ada):
    batch, d_model = c.shape
    n_out = w_ada.shape[1]
    n_blk = n_out // 6
    return pl.pallas_call(
        _adaln_kernel,
        out_shape=jax.ShapeDtypeStruct((batch, n_out), F32),
        grid=(n_out // n_blk,),
        in_specs=[pl.BlockSpec((batch, d_model), lambda j: (0, 0)),
                  pl.BlockSpec((d_model, n_blk), lambda j: (0, j)),
                  pl.BlockSpec((1, n_blk), lambda j: (0, j))],
        out_specs=pl.BlockSpec((batch, n_blk), lambda j: (0, j)),
        compiler_params=pltpu.CompilerParams(dimension_semantics=("arbitrary",),
                                             vmem_limit_bytes=V7X_VMEM_LIMIT_BYTES),
        name="adaln_mod",
    )(c, w_ada, b_ada.reshape(1, n_out))


def _t5_bucket_table():
    qi = np.arange(BLOCK)[:, None]
    si = np.arange(2 * BLOCK)[None, :]
    n = np.maximum(qi + BLOCK - si, 0)
    max_exact = N_BUCKETS // 2
    nf = np.maximum(n, max_exact).astype(np.float64)
    large = max_exact + (np.log(nf / max_exact) / math.log(MAX_DISTANCE / max_exact)
                         * (N_BUCKETS - max_exact)).astype(np.int32)
    large = np.minimum(large, N_BUCKETS - 1)
    return np.where(n < max_exact, n, large).astype(np.int32)


def _tables_kernel(relb_ref, bucket_ref, ws_ref, tb_ref, wcat_ref, *, n_heads):
    bucket = bucket_ref[...]
    qi = lax.broadcasted_iota(jnp.int32, bucket.shape, 0)
    si = lax.broadcasted_iota(jnp.int32, bucket.shape, 1)
    dist = qi + BLOCK - si
    group = n_heads // N_KV_HEADS
    for h in range(n_heads):
        bias = jnp.zeros(bucket.shape, F32)
        for b in range(N_BUCKETS):
            bias = jnp.where(bucket == b, relb_ref[b * n_heads + h] * LOG2E, bias)
        kv, c, half = h // group, (h % group) // 2, h % 2
        rows = slice(BLOCK * c, BLOCK * (c + 1))
        in_window = jnp.where(dist >= 0, jnp.where(dist < BLOCK, bias, NEG_INF), NEG_INF)
        tb_ref[0, kv, half, rows, :] = jnp.where(si >= BLOCK, in_window, NEG_INF)
        tb_ref[1, kv, half, rows, :] = in_window
    ti = lax.broadcasted_iota(jnp.int32, (BLOCK, BLOCK), 0)
    sj = lax.broadcasted_iota(jnp.int32, (BLOCK, BLOCK), 1)
    causal = ti >= sj
    for j in range(ws_ref.shape[0] // 2):
        wcat_ref[j, :, 0:BLOCK] = jnp.where(causal, ws_ref[2 * j], 0.0).astype(BF16)
        wcat_ref[j, :, BLOCK:2 * BLOCK] = jnp.where(causal, ws_ref[2 * j + 1], 0.0).astype(BF16)


def _mixer_tables(rel_bias, w_s):
    n_heads = rel_bias.shape[1]
    n_groups = w_s.shape[0]
    bucket = jnp.asarray(_t5_bucket_table())
    vmem = pl.BlockSpec(memory_space=pltpu.VMEM)
    return pl.pallas_call(
        functools.partial(_tables_kernel, n_heads=n_heads),
        out_shape=(jax.ShapeDtypeStruct((2, N_KV_HEADS, 2, 2 * BLOCK, 2 * BLOCK), F32),
                   jax.ShapeDtypeStruct((n_groups // 2, BLOCK, 2 * BLOCK), BF16)),
        in_specs=[pl.BlockSpec(memory_space=pltpu.SMEM), vmem, vmem],
        out_specs=(vmem, vmem),
        compiler_params=pltpu.CompilerParams(vmem_limit_bytes=V7X_VMEM_LIMIT_BYTES),
        name="mixer_tables",
    )(rel_bias.reshape(-1), bucket, w_s)


def _layer_kernel(x_ref, mod_ref, w_in_ref, b_in_ref, tb_ref, sinks_ref, glg_ref, glb_ref,
                  wcat_ref, bs_ref, aog_ref, gog_ref, w_out_ref, l1g_ref, l1b_ref,
                  wgu_ref, wd_ref, l2g_ref, l2b_ref, o_ref, kv_ref, x1_ref, act_ref,
                  *, alpha, n_tiles, tiles_per_seq):
    tm = x_ref.shape[0]
    g = pl.program_id(0)
    mixer_tile = jnp.minimum(g, n_tiles - 1)
    ffn_tile = jnp.maximum(g - 1, 0)
    seq_start = lax.rem(mixer_tile, tiles_per_seq) == 0
    mixer_refs = (x_ref, mod_ref.at[lax.div(mixer_tile, tiles_per_seq)], w_in_ref, b_in_ref, tb_ref,
                  sinks_ref, glg_ref, glb_ref, wcat_ref, bs_ref, aog_ref, gog_ref, w_out_ref,
                  l1g_ref, l1b_ref, x1_ref, kv_ref)
    ffn_refs = (x1_ref, mod_ref.at[lax.div(ffn_tile, tiles_per_seq)], wgu_ref, wd_ref, l2g_ref,
                l2b_ref, o_ref, act_ref)

    @pl.when(g == 0)
    def _():
        x1_ref[...] = jnp.zeros(x1_ref.shape, F32)

    @pl.when(seq_start)
    def _():
        kv_ref[:, :, :, 0:BLOCK, :] = jnp.zeros((2, N_KV_HEADS, 2, BLOCK, LANES), BF16)

    subs = range(0, tm, SUB_TILE)
    chains = [chain for row0 in subs for chain in (_ffn_chain(row0, ffn_refs, alpha),
                                                   _mixer_chain(row0, seq_start, mixer_refs, alpha))]
    _interleave(chains, [start for k in range(len(subs))
                         for start in (k * FFN_STAGGER, k * MIXER_STAGGER)])

    kv_ref[:, :, :, 0:BLOCK, :] = kv_ref[:, :, :, tm:tm + BLOCK, :]


def _mixer_chain(row0, seq_start, refs, alpha):
    (x_ref, mod_ref, w_in_ref, b_in_ref, tb_ref, sinks_ref, glg_ref, glb_ref, wcat_ref,
     bs_ref, aog_ref, gog_ref, w_out_ref, l1g_ref, l1b_ref, o_ref, kv_ref) = refs
    tm = SUB_TILE
    n_blk = tm // BLOCK
    d_model = x_ref.shape[1]
    attn_w = aog_ref.shape[1]
    kv_w = N_KV_HEADS * HEAD_DIM
    gm_w = gog_ref.shape[1]
    rows = slice(row0, row0 + tm)
    low = lax.broadcasted_iota(jnp.int32, (tm, LANES), 1) < HEAD_DIM
    inv_g = 1.0 / GMLP_GROUP_DIM

    sh1, sc1, g1 = mod_ref[0:1, :], mod_ref[1:2, :], mod_ref[2:3, :]
    h = (x_ref[rows, :] * (1.0 + sc1) + sh1).astype(BF16)
    yield

    def proj(c0, width):
        return (jnp.dot(h, w_in_ref[:, c0:c0 + width], preferred_element_type=F32)
                + b_in_ref[:, c0:c0 + width])

    q_tiles = []
    for c0 in range(0, attn_w, MXU_COLS):
        q_tiles += _lane_tiles((proj(c0, MXU_COLS) * (LOG2E * HEAD_DIM ** -0.5)).astype(BF16))
        yield

    kv_new = proj(attn_w, 2 * kv_w)
    cur = slice(BLOCK + row0, BLOCK + row0 + tm)
    for which in range(2):
        val = kv_new[:, which * kv_w:(which + 1) * kv_w]
        rolled = pltpu.roll(val, HEAD_DIM, axis=1)
        kv_ref[which, 0, 0, cur, :] = jnp.where(low, val, 0.0).astype(BF16)
        kv_ref[which, 0, 1, cur, :] = jnp.where(low, 0.0, rolled).astype(BF16)
        kv_ref[which, 1, 0, cur, :] = jnp.where(low, rolled, 0.0).astype(BF16)
        kv_ref[which, 1, 1, cur, :] = jnp.where(low, 0.0, val).astype(BF16)
    yield

    gu0 = attn_w + 2 * kv_w
    gv0 = gu0 + gm_w
    rhs_tiles = []
    for c0 in range(gv0, gv0 + gm_w, MXU_COLS):
        for jj, xj in enumerate(_lane_tiles(_gelu_tanh(proj(c0, MXU_COLS)))):
            cj = slice(c0 - gv0 + jj * LANES, c0 - gv0 + (jj + 1) * LANES)
            s_low = jnp.sum(jnp.where(low, xj, 0.0), axis=-1, keepdims=True)
            s_all = jnp.sum(xj, axis=-1, keepdims=True)
            dj = xj - jnp.where(low, s_low, s_all - s_low) * inv_g
            d2 = dj * dj
            v_low = jnp.sum(jnp.where(low, d2, 0.0), axis=-1, keepdims=True)
            v_all = jnp.sum(d2, axis=-1, keepdims=True)
            var = jnp.where(low, v_low, v_all - v_low) * inv_g
            vn = dj * lax.rsqrt(var + LN_EPS) * glg_ref[:, cj] + glb_ref[:, cj]
            vn_low = jnp.where(low, vn, 0.0).astype(BF16)
            vn_high = jnp.where(low, 0.0, vn).astype(BF16)
            rhs_tiles.append(jnp.concatenate(
                [jnp.concatenate([vn_low[c * BLOCK:(c + 1) * BLOCK],
                                  vn_high[c * BLOCK:(c + 1) * BLOCK]], axis=0)
                 for c in range(n_blk)], axis=1))
        yield

    u_tiles = []
    for c0 in range(gu0, gu0 + gm_w, MXU_COLS):
        u_tiles += _lane_tiles(_gelu_tanh(proj(c0, MXU_COLS)))
        yield

    top = lax.broadcasted_iota(jnp.int32, (2 * BLOCK, 1), 0) < BLOCK
    nt = (((1,), (1,)), ((), ()))
    heads = [(t, kvh, half) for t in range(n_blk) for kvh in range(N_KV_HEADS) for half in range(2)]
    n_lane = gm_w // LANES
    scores, probs, att_parts, gm_tiles = {}, {}, {}, []
    for unit in range(-(-len(heads) // QK_PER_UNIT) + PV_LAG):
        qk_now = heads[unit * QK_PER_UNIT:(unit + 1) * QK_PER_UNIT]
        pv_now = heads[(unit - PV_LAG) * QK_PER_UNIT:(unit - PV_LAG + 1) * QK_PER_UNIT] \
            if unit >= PV_LAG else []
        g_now = list(range(unit * G_PER_UNIT, min((unit + 1) * G_PER_UNIT, n_lane)))
        for t, kvh, half in qk_now:
            r0 = t * BLOCK
            q2 = jnp.concatenate([q_tiles[2 * kvh][r0:r0 + BLOCK],
                                  q_tiles[2 * kvh + 1][r0:r0 + BLOCK]], axis=0)
            kwin = kv_ref[0, kvh, half, row0 + r0:row0 + r0 + 2 * BLOCK, :]
            scores[t, kvh, half] = lax.dot_general(q2, kwin, nt, preferred_element_type=F32)
        pv_out = {}
        for t, kvh, half in pv_now:
            p, denom = probs.pop((t, kvh, half))
            vwin = kv_ref[1, kvh, half, row0 + t * BLOCK:row0 + (t + 2) * BLOCK, :]
            pv_out[t, kvh, half] = (jnp.dot(p, vwin, preferred_element_type=F32), denom)
        mixed_now = {j: jnp.dot(wcat_ref[j], rhs_tiles[j], preferred_element_type=F32)
                     for j in g_now}
        for t, kvh, half in qk_now:
            first = jnp.where(seq_start, 0, 1) if row0 + t * BLOCK == 0 else 1
            s = scores.pop((t, kvh, half)) + tb_ref[first, kvh, half]
            h0 = 4 * kvh + half
            sink = jnp.where(top, sinks_ref[h0] * LOG2E, sinks_ref[h0 + 2] * LOG2E)
            m = jnp.maximum(jnp.max(s, axis=-1, keepdims=True), sink)
            p = jnp.exp2(s - m)
            denom = jnp.sum(p, axis=-1, keepdims=True) + jnp.exp2(sink - m)
            probs[t, kvh, half] = (p.astype(BF16), denom)
        for key, (o, denom) in pv_out.items():
            att_parts[key] = o / denom
        for j in g_now:
            cj = slice(LANES * j, LANES * (j + 1))
            mixed = jnp.concatenate(_lane_tiles(mixed_now[j]), axis=0)
            bias = jnp.concatenate([bs_ref[:, cj]] * n_blk, axis=0)
            gm_tiles.append(u_tiles[j] * (mixed + bias))
        yield
    assert len(gm_tiles) == n_lane and not probs
    attn_blocks = []
    for t in range(n_blk):
        cols = []
        for kvh in range(N_KV_HEADS):
            att = att_parts[t, kvh, 0] + att_parts[t, kvh, 1]
            cols += [att[0:BLOCK], att[BLOCK:2 * BLOCK]]
        attn_blocks.append(jnp.concatenate(cols, axis=1))
    attn = jnp.concatenate(attn_blocks, axis=0)
    gm = jnp.concatenate(gm_tiles, axis=1)
    mixed_all = jnp.concatenate([_rms_norm(attn, aog_ref[...]), _rms_norm(gm, gog_ref[...])],
                                axis=1).astype(BF16)
    for _ in range(NORM_LAG):
        yield

    z_parts = []
    for c0 in range(0, d_model, MXU_COLS):
        cs = slice(c0, c0 + MXU_COLS)
        y = jnp.dot(mixed_all, w_out_ref[:, cs], preferred_element_type=F32)
        z_parts.append(alpha * x_ref[rows, cs] + g1[:, cs] * y)
        yield
    o_ref[rows, :] = _layer_norm(jnp.concatenate(z_parts, axis=1), l1g_ref[...], l1b_ref[...])
    yield


def _ffn_chain(row0, refs, alpha):
    x_ref, mod_ref, wgu_ref, wd_ref, l2g_ref, l2b_ref, o_ref, act_ref = refs
    d_ff, d_model = wd_ref.shape
    rows = slice(row0, row0 + SUB_TILE)
    x = x_ref[rows, :]
    sh2, sc2, g2 = mod_ref[3:4, :], mod_ref[4:5, :], mod_ref[5:6, :]
    h = (x * (1.0 + sc2) + sh2).astype(BF16)
    yield
    for c0 in range(0, d_ff, FF_CHUNK):
        gate = jnp.dot(h, wgu_ref[:, c0:c0 + FF_CHUNK], preferred_element_type=F32)
        up = jnp.dot(h, wgu_ref[:, d_ff + c0:d_ff + c0 + FF_CHUNK], preferred_element_type=F32)
        act_ref[rows, c0:c0 + FF_CHUNK] = (_silu(gate) * up).astype(BF16)
        yield
    z_parts = []
    for c0 in range(0, d_model, MXU_COLS):
        cs = slice(c0, c0 + MXU_COLS)
        y = jnp.dot(act_ref[rows, :], wd_ref[:, cs], preferred_element_type=F32)
        z_parts.append(alpha * x[:, cs] + g2[:, cs] * y)
        yield
    o_ref[rows, :] = _layer_norm(jnp.concatenate(z_parts, axis=1), l2g_ref[...], l2b_ref[...])
    yield


def _layer(x, mod, w_in, b_in, tb, sinks, gln_g, gln_b, wcat, bs, aog, gog, w_out, l1g, l1b,
           w_gate_up, w_down, l2g, l2b, *, alpha):
    batch, seq, d_model = x.shape
    d_ff = w_down.shape[0]
    tm = LAYER_TILE
    n_tiles = batch * seq // tm
    row = lambda a: a.reshape(1, -1)
    consts = [mod, w_in, row(b_in), tb, None, row(gln_g), row(gln_b), wcat, bs, row(aog), row(gog),
              w_out, row(l1g), row(l1b), w_gate_up, w_down, row(l2g), row(l2b)]
    const_specs = [pl.BlockSpec(memory_space=pltpu.SMEM) if a is None else _const_spec(a.shape)
                   for a in consts]
    consts[4] = sinks
    out = pl.pallas_call(
        functools.partial(_layer_kernel, alpha=alpha, n_tiles=n_tiles, tiles_per_seq=seq // tm),
        out_shape=jax.ShapeDtypeStruct((n_tiles, tm, d_model), F32),
        grid=(n_tiles + 1,),
        in_specs=[pl.BlockSpec((None, tm, d_model), lambda g: (jnp.minimum(g, n_tiles - 1), 0, 0))]
                 + const_specs,
        out_specs=pl.BlockSpec((None, tm, d_model), lambda g: (jnp.maximum(g - 1, 0), 0, 0)),
        scratch_shapes=[pltpu.VMEM((2, N_KV_HEADS, 2, BLOCK + tm, LANES), BF16),
                        pltpu.VMEM((tm, d_model), F32),
                        pltpu.VMEM((tm, d_ff), BF16)],
        compiler_params=pltpu.CompilerParams(dimension_semantics=("arbitrary",),
                                             vmem_limit_bytes=V7X_VMEM_LIMIT_BYTES),
        name="layer",
    )(x.reshape(n_tiles, tm, d_model), *consts)
    return out.reshape(batch, seq, d_model)


def kernel(x, c, rel_bias, w_ada, b_ada, w_in, b_in, attn_sinks, gmlp_ln_g, gmlp_ln_b,
           gmlp_w_s, gmlp_b_s, attn_out_g, gmlp_out_g, w_out, ln1_g, ln1_b,
           w_gate_up, w_down, ln2_g, ln2_b):
    batch, seq, d_model = x.shape
    depth = w_in.shape[0]
    assert seq % LAYER_TILE == 0 and LAYER_TILE % SUB_TILE == 0 and SUB_TILE % BLOCK == 0
    assert w_down.shape[1] % FF_CHUNK == 0
    alpha = (2 * depth) ** 0.25
    for layer in range(depth):
        mod = _adaln_mod(c, w_ada[layer], b_ada[layer]).reshape(batch, 6, d_model)
        tb, wcat = _mixer_tables(rel_bias, gmlp_w_s[layer])
        bs = jnp.repeat(gmlp_b_s[layer].T, GMLP_GROUP_DIM, axis=1)
        x = _layer(x, mod, w_in[layer].astype(BF16), b_in[layer], tb, attn_sinks[layer],
                   gmlp_ln_g[layer], gmlp_ln_b[layer], wcat, bs, attn_out_g[layer],
                   gmlp_out_g[layer], w_out[layer].astype(BF16), ln1_g[layer], ln1_b[layer],
                   w_gate_up[layer].astype(BF16), w_down[layer].astype(BF16),
                   ln2_g[layer], ln2_b[layer], alpha=alpha)
    return x
```
